```python
import math
import jax, jax.numpy as jnp
from jax import lax
import numpy as np

D_MODEL = 1024
BATCH = 4
SEQ = 4096
DEPTH = 1
DEC_BATCH = 128
DEC_SEQ = 4
PAST_LEN = 8192
PAGE_SIZE = 128

N_HEADS = 16
N_KV_HEADS = 2
HEAD_DIM = 64
GQA = N_HEADS // N_KV_HEADS
WINDOW = 128
BLOCK = 128
SSM_WIDTH = D_MODEL // 2
SSM_GROUP = 16
SSM_GROUPS = SSM_WIDTH // SSM_GROUP
SSM_STATE = 64
N_EXPERTS = 32
TOP_K = 4
D_FF = D_MODEL
PLE_DIM = 256
SWIGLU_LIMIT = 7.0
SWIGLU_ALPHA = 1.702
RMS_EPS = 1e-6
Q_W = N_HEADS * HEAD_DIM
KV_W = N_KV_HEADS * HEAD_DIM
IN_COLS = Q_W + 2 * KV_W + SSM_WIDTH + 2 * D_MODEL
IN_SPLITS = (Q_W, Q_W + KV_W, Q_W + 2 * KV_W, Q_W + 2 * KV_W + SSM_WIDTH, Q_W + 2 * KV_W + SSM_WIDTH + D_MODEL)

kernel_name = 'hybrid_swa_s5_moe_ple_step'


def rms_norm(x, g):
    xf = x.astype(jnp.float32)
    xf = xf * lax.rsqrt(jnp.mean(xf * xf, axis=-1, keepdims=True) + RMS_EPS)
    return (xf * g.astype(jnp.float32)).astype(x.dtype)


def alibi_slopes():
    return jnp.exp2(-8.0 * (jnp.arange(N_HEADS, dtype=jnp.float32) + 1.0) / N_HEADS)


def sink_attention(q, k, v, dist, mask, sinks):
    s = jnp.einsum('...qkgd,...skd->...kgqs', q, k).astype(jnp.float32) * (HEAD_DIM ** -0.5)
    slopes = alibi_slopes().reshape(N_KV_HEADS, GQA, 1, 1)
    s = jnp.where(mask, s - slopes * dist.astype(jnp.float32), -jnp.inf)
    sink = sinks.astype(jnp.float32).reshape(N_KV_HEADS, GQA, 1, 1)
    m = jnp.maximum(jnp.max(s, axis=-1, keepdims=True), sink)
    e = jnp.exp(s - m)
    probs = e / (jnp.sum(e, axis=-1, keepdims=True) + jnp.exp(sink - m))
    return jnp.einsum('...kgqs,...skd->...qkgd', probs.astype(v.dtype), v)


def attn_prompt(q, k, v, sinks):
    bsz, L = q.shape[:2]
    nb = L // BLOCK
    qb = q.reshape(bsz, nb, BLOCK, N_KV_HEADS, GQA, HEAD_DIM)

    def band(z):
        zb = z.reshape(bsz, nb, BLOCK, N_KV_HEADS, HEAD_DIM)
        prev = jnp.pad(zb, ((0, 0), (1, 0), (0, 0), (0, 0), (0, 0)))[:, :nb]
        return jnp.concatenate([prev, zb], axis=2)

    qi = jnp.arange(BLOCK)[:, None]
    kj = jnp.arange(2 * BLOCK)[None, :]
    dist = BLOCK + qi - kj
    kpos = (jnp.arange(nb)[:, None, None] - 1) * BLOCK + kj[None]
    mask = (dist >= 0) & (dist <= WINDOW) & (kpos >= 0)
    o = sink_attention(qb, band(k), band(v), dist, mask[:, None, None], sinks)
    return o.reshape(bsz, L, Q_W)


def attn_sample(q, k, v, cache_k, cache_v, sinks):
    bsz, s_new = q.shape[:2]
    kk = jnp.concatenate([cache_k.astype(k.dtype), k], axis=1)
    vv = jnp.concatenate([cache_v.astype(v.dtype), v], axis=1)
    qi = jnp.arange(s_new)[:, None]
    kj = jnp.arange(WINDOW + s_new)[None, :]
    dist = WINDOW + qi - kj
    mask = (dist >= 0) & (dist <= WINDOW)
    o = sink_attention(q, kk, vv, dist, mask, sinks)
    return o.reshape(bsz, s_new, Q_W), kk[:, -WINDOW:], vv[:, -WINDOW:]


def ssm_scan(u, h0_re, h0_im, a_re, a_im, log_dt, b_re, b_im, c_re, c_im, d_skip):
    f32 = jnp.float32
    u = u.astype(f32)
    a_re, a_im, b_re, b_im = a_re.astype(f32), a_im.astype(f32), b_re.astype(f32), b_im.astype(f32)
    c_re, c_im = c_re.astype(f32), c_im.astype(f32)
    h0_re, h0_im = h0_re.astype(f32), h0_im.astype(f32)
    bsz, L = u.shape[:2]
    dt = jnp.exp(log_dt.astype(f32))[:, None]
    mag = jnp.exp(dt * a_re)
    ab_re = mag * jnp.cos(dt * a_im)
    ab_im = mag * jnp.sin(dt * a_im)
    den = a_re * a_re + a_im * a_im
    nr = ab_re - 1.0
    f_re = ((nr * a_re + ab_im * a_im) / den)[..., None]
    f_im = ((ab_im * a_re - nr * a_im) / den)[..., None]
    bb_re = f_re * b_re - f_im * b_im
    bb_im = f_re * b_im + f_im * b_re
    ug = u.reshape(bsz, L, SSM_GROUPS, SSM_GROUP)
    x_re = jnp.einsum('blgc,gnc->blgn', ug, bb_re)
    x_im = jnp.einsum('blgc,gnc->blgn', ug, bb_im)
    a_r = jnp.broadcast_to(ab_re, x_re.shape)
    a_i = jnp.broadcast_to(ab_im, x_re.shape)

    def combine(e1, e2):
        a1r, a1i, b1r, b1i = e1
        a2r, a2i, b2r, b2i = e2
        return (a2r * a1r - a2i * a1i, a2r * a1i + a2i * a1r,
                a2r * b1r - a2i * b1i + b2r, a2r * b1i + a2i * b1r + b2i)

    _, _, h_re, h_im = lax.associative_scan(combine, (a_r, a_i, x_re, x_im), axis=1)
    tt = jnp.arange(1, L + 1, dtype=f32)[:, None, None] * dt[None]
    pmag = jnp.exp(tt * a_re)
    p_re = pmag * jnp.cos(tt * a_im)
    p_im = pmag * jnp.sin(tt * a_im)
    h_re = h_re + p_re * h0_re[:, None] - p_im * h0_im[:, None]
    h_im = h_im + p_re * h0_im[:, None] + p_im * h0_re[:, None]
    y = jnp.einsum('blgn,gcn->blgc', h_re, c_re) - jnp.einsum('blgn,gcn->blgc', h_im, c_im)
    y = y.reshape(bsz, L, SSM_WIDTH) + d_skip.astype(f32) * u
    return y, h_re[:, -1], h_im[:, -1]


def moe(h, w_router, b_router, w_gate, b_gate, w_up, b_up, w_down, b_down):
    shp = h.shape
    hf = h.reshape(-1, D_MODEL)
    logits = (hf @ w_router + b_router).astype(jnp.float32)
    top_v, top_i = lax.top_k(logits, TOP_K)
    probs = jax.nn.softmax(top_v, axis=-1)
    comb = jnp.einsum('nk,nke->ne', probs, jax.nn.one_hot(top_i, N_EXPERTS, dtype=jnp.float32)).astype(h.dtype)
    out = jnp.zeros_like(hf)
    for e in range(N_EXPERTS):
        g = jnp.minimum(hf @ w_gate[e] + b_gate[e], SWIGLU_LIMIT)
        up = jnp.clip(hf @ w_up[e] + b_up[e], -SWIGLU_LIMIT, SWIGLU_LIMIT)
        act = g * jax.nn.sigmoid(SWIGLU_ALPHA * g) * (up + 1.0)
        out = out + comb[:, e:e + 1] * (act @ w_down[e] + b_down[e])
    return out.reshape(shp)


def trunk_layer(x, p, is_prompt, cache_k, cache_v, s_re, s_im, lw):
    bsz, L, _ = x.shape
    h = rms_norm(x, lw['norm_mix'])
    q, k, v, u, g_attn, g_ssm = jnp.split(h @ lw['w_in'], IN_SPLITS, axis=-1)
    q = rms_norm(q.reshape(bsz, L, N_HEADS, HEAD_DIM), lw['q_norm']).reshape(bsz, L, N_KV_HEADS, GQA, HEAD_DIM)
    k = rms_norm(k.reshape(bsz, L, N_KV_HEADS, HEAD_DIM), lw['k_norm'])
    v = v.reshape(bsz, L, N_KV_HEADS, HEAD_DIM)
    if is_prompt:
        attn = attn_prompt(q, k, v, lw['attn_sinks'])
        new_k, new_v = k[:, -WINDOW:], v[:, -WINDOW:]
        h0_re = jnp.zeros((bsz, SSM_GROUPS, SSM_STATE), jnp.float32)
        h0_im = h0_re
    else:
        attn, new_k, new_v = attn_sample(q, k, v, cache_k, cache_v, lw['attn_sinks'])
        h0_re, h0_im = s_re, s_im
    y_ssm, hT_re, hT_im = ssm_scan(u, h0_re, h0_im, lw['ssm_a_re'], lw['ssm_a_im'], lw['ssm_log_dt'],
                                   lw['ssm_b_re'], lw['ssm_b_im'], lw['ssm_c_re'], lw['ssm_c_im'], lw['ssm_d'])
    z = jax.nn.gelu(y_ssm.astype(x.dtype))
    z = z * jax.nn.sigmoid(z @ lw['w_ssm_glu'])
    merged = (jax.nn.sigmoid(g_attn) * (attn @ lw['w_attn_branch'])
              + jax.nn.sigmoid(g_ssm) * (z @ lw['w_ssm_branch']))
    x = x + merged @ lw['w_out']
    x = x + moe(rms_norm(x, lw['norm_moe']), lw['w_router'], lw['b_router'], lw['w_gate'], lw['b_gate'],
                lw['w_up'], lw['b_up'], lw['w_down'], lw['b_down'])
    gate = jax.nn.sigmoid(rms_norm(x, lw['norm_ple']) @ lw['w_ple_gate'])
    x = x + (p @ lw['w_ple_proj']) * gate
    dt_state = x.dtype if s_re is None else s_re.dtype
    return x, new_k, new_v, hT_re.astype(dt_state), hT_im.astype(dt_state)


def setup_inputs(seed: int = 0) -> dict:
    key = jax.random.key(seed)
    ks = iter(jax.random.split(key, 48))
    f32 = jnp.float32

    def nrm(shape, scale):
        return jax.random.normal(next(ks), shape, f32) * scale

    def gain(shape):
        return 1.0 + nrm(shape, 0.02)

    n_idx = jnp.arange(SSM_STATE, dtype=f32)
    return {
        'x_prompt': nrm((BATCH, SEQ, D_MODEL), 1.0),
        'x_sample': nrm((DEC_BATCH, DEC_SEQ, D_MODEL), 1.0),
        'p_prompt': nrm((DEPTH, BATCH, SEQ, PLE_DIM), 1.0),
        'p_sample': nrm((DEPTH, DEC_BATCH, DEC_SEQ, PLE_DIM), 1.0),
        'cache_k': nrm((DEPTH, DEC_BATCH, WINDOW, N_KV_HEADS, HEAD_DIM), 1.0),
        'cache_v': nrm((DEPTH, DEC_BATCH, WINDOW, N_KV_HEADS, HEAD_DIM), 1.0),
        'state_ssm_re': nrm((DEPTH, DEC_BATCH, SSM_GROUPS, SSM_STATE), 0.5),
        'state_ssm_im': nrm((DEPTH, DEC_BATCH, SSM_GROUPS, SSM_STATE), 0.5),
        'norm_mix': gain((DEPTH, D_MODEL)),
        'w_in': nrm((DEPTH, D_MODEL, IN_COLS), D_MODEL ** -0.5),
        'q_norm': gain((DEPTH, HEAD_DIM)),
        'k_norm': gain((DEPTH, HEAD_DIM)),
        'attn_sinks': nrm((DEPTH, N_HEADS), 0.5),
        'w_attn_branch': nrm((DEPTH, Q_W, D_MODEL), Q_W ** -0.5),
        'ssm_a_re': -0.5 + nrm((DEPTH, SSM_GROUPS, SSM_STATE), 0.01),
        'ssm_a_im': jnp.pi * n_idx + nrm((DEPTH, SSM_GROUPS, SSM_STATE), 0.01),
        'ssm_log_dt': jax.random.uniform(next(ks), (DEPTH, SSM_GROUPS), f32, math.log(1e-3), math.log(1e-1)),
        'ssm_b_re': nrm((DEPTH, SSM_GROUPS, SSM_STATE, SSM_GROUP), (2.0 * SSM_GROUP) ** -0.5),
        'ssm_b_im': nrm((DEPTH, SSM_GROUPS, SSM_STATE, SSM_GROUP), (2.0 * SSM_GROUP) ** -0.5),
        'ssm_c_re': nrm((DEPTH, SSM_GROUPS, SSM_GROUP, SSM_STATE), (2.0 * SSM_STATE) ** -0.5),
        'ssm_c_im': nrm((DEPTH, SSM_GROUPS, SSM_GROUP, SSM_STATE), (2.0 * SSM_STATE) ** -0.5),
        'ssm_d': nrm((DEPTH, SSM_WIDTH), 1.0),
        'w_ssm_glu': nrm((DEPTH, SSM_WIDTH, SSM_WIDTH), SSM_WIDTH ** -0.5),
        'w_ssm_branch': nrm((DEPTH, SSM_WIDTH, D_MODEL), SSM_WIDTH ** -0.5),
        'w_out': nrm((DEPTH, D_MODEL, D_MODEL), D_MODEL ** -0.5),
        'norm_moe': gain((DEPTH, D_MODEL)),
        'w_router': nrm((DEPTH, D_MODEL, N_EXPERTS), D_MODEL ** -0.5),
        'b_router': nrm((DEPTH, N_EXPERTS), 0.01),
        'w_gate': nrm((DEPTH, N_EXPERTS, D_MODEL, D_FF), D_MODEL ** -0.5),
        'b_gate': nrm((DEPTH, N_EXPERTS, D_FF), 0.01),
        'w_up': nrm((DEPTH, N_EXPERTS, D_MODEL, D_FF), D_MODEL ** -0.5),
        'b_up': nrm((DEPTH, N_EXPERTS, D_FF), 0.01),
        'w_down': nrm((DEPTH, N_EXPERTS, D_FF, D_MODEL), D_FF ** -0.5),
        'b_down': nrm((DEPTH, N_EXPERTS, D_MODEL), 0.01),
        'norm_ple': gain((DEPTH, D_MODEL)),
        'w_ple_gate': nrm((DEPTH, D_MODEL, D_MODEL), D_MODEL ** -0.5),
        'w_ple_proj': nrm((DEPTH, PLE_DIM, D_MODEL), PLE_DIM ** -0.5),
    }


def reference(x_prompt, x_sample, p_prompt, p_sample, cache_k, cache_v, state_ssm_re, state_ssm_im,
              norm_mix, w_in, q_norm, k_norm, attn_sinks, w_attn_branch,
              ssm_a_re, ssm_a_im, ssm_log_dt, ssm_b_re, ssm_b_im, ssm_c_re, ssm_c_im, ssm_d,
              w_ssm_glu, w_ssm_branch, w_out, norm_moe, w_router, b_router,
              w_gate, b_gate, w_up, b_up, w_down, b_down, norm_ple, w_ple_gate, w_ple_proj):
    xp, xs = x_prompt, x_sample
    kp_l, vp_l, srp_l, sip_l = [], [], [], []
    ks_l, vs_l, srs_l, sis_l = [], [], [], []
    for i in range(DEPTH):
        lw = dict(norm_mix=norm_mix[i], w_in=w_in[i], q_norm=q_norm[i], k_norm=k_norm[i],
                  attn_sinks=attn_sinks[i], w_attn_branch=w_attn_branch[i],
                  ssm_a_re=ssm_a_re[i], ssm_a_im=ssm_a_im[i], ssm_log_dt=ssm_log_dt[i],
                  ssm_b_re=ssm_b_re[i], ssm_b_im=ssm_b_im[i], ssm_c_re=ssm_c_re[i], ssm_c_im=ssm_c_im[i],
                  ssm_d=ssm_d[i], w_ssm_glu=w_ssm_glu[i], w_ssm_branch=w_ssm_branch[i], w_out=w_out[i],
                  norm_moe=norm_moe[i], w_router=w_router[i], b_router=b_router[i],
                  w_gate=w_gate[i], b_gate=b_gate[i], w_up=w_up[i], b_up=b_up[i],
                  w_down=w_down[i], b_down=b_down[i], norm_ple=norm_ple[i],
                  w_ple_gate=w_ple_gate[i], w_ple_proj=w_ple_proj[i])
        xp, kp, vp, srp, sip = trunk_layer(xp, p_prompt[i], True, None, None, None, None, lw)
        xs, ksn, vsn, srs, sis = trunk_layer(xs, p_sample[i], False, cache_k[i], cache_v[i],
                                             state_ssm_re[i], state_ssm_im[i], lw)
        kp_l.append(kp); vp_l.append(vp); srp_l.append(srp); sip_l.append(sip)
        ks_l.append(ksn); vs_l.append(vsn); srs_l.append(srs); sis_l.append(sis)
    return (xp, xs,
            jnp.stack(kp_l), jnp.stack(vp_l), jnp.stack(srp_l), jnp.stack(sip_l),
            jnp.stack(ks_l), jnp.stack(vs_l), jnp.stack(srs_l), jnp.stack(sis_l))
```

```python
import functools

import jax
import jax.numpy as jnp
from jax import lax
from jax.experimental import pallas as pl
from jax.experimental.pallas import tpu as pltpu

D_MODEL = 1024
N_HEADS = 16
N_KV_HEADS = 2
HEAD_DIM = 64
GQA = N_HEADS // N_KV_HEADS
WINDOW = 128
BLOCK = 128
SSM_WIDTH = 512
SSM_GROUP = 16
SSM_GROUPS = SSM_WIDTH // SSM_GROUP
SSM_STATE = 64
N_STATE = SSM_GROUPS * SSM_STATE
N_EXPERTS = 32
TOP_K = 4
D_FF = 1024
PLE_DIM = 256
SWIGLU_LIMIT = 7.0
SWIGLU_ALPHA = 1.702
RMS_EPS = 1e-6
Q_W = N_HEADS * HEAD_DIM
KV_W = N_KV_HEADS * HEAD_DIM
U0 = Q_W + 2 * KV_W
GA0 = U0 + SSM_WIDTH
GS0 = GA0 + D_MODEL
IN_COLS = GS0 + D_MODEL

LANES = 128
SUBLANES = 8
HALF_STATE = N_STATE // 2
N_SLABS = HALF_STATE // LANES
PACK_W = D_MODEL // 2

TM_PROJ = 512
T_SCAN = 128
T_ROWS = 256
SEQ_BLK = 16
VMEM_LIMIT = 56 * 1024 * 1024

F32 = jnp.float32
BF16 = jnp.bfloat16
U32 = jnp.uint32
I32 = jnp.int32
_NT = (((1,), (1,)), ((), ()))


def _dot(a, b):
    return jnp.dot(a, b, preferred_element_type=F32)


def _dot_nt(a, b):
    return lax.dot_general(a, b, _NT, preferred_element_type=F32)


def _rms(x, g):
    return x * lax.rsqrt(jnp.mean(x * x, axis=-1, keepdims=True) + RMS_EPS) * g


def _pack_bf16_pairs(x):
    w = x.shape[-1] // 2
    lo = lax.bitcast_convert_type(x[:, :w].astype(BF16).astype(F32), U32)
    hi = lax.bitcast_convert_type(x[:, w:].astype(BF16).astype(F32), U32)
    return (hi & jnp.uint32(0xFFFF0000)) | (lo >> 16)


def _unpack_bf16_pairs(w):
    lo = lax.bitcast_convert_type(w << 16, F32)
    hi = lax.bitcast_convert_type(w & jnp.uint32(0xFFFF0000), F32)
    return jnp.concatenate([lo, hi], axis=-1)


def _params(*sem):
    return pltpu.CompilerParams(dimension_semantics=sem, vmem_limit_bytes=VMEM_LIMIT)


def _full(shape):
    return pl.BlockSpec(shape, lambda *_: (0,) * len(shape))


def _proj_kernel(x_ref, nw_ref, w_ref, gq_ref, gk_ref, qg_ref, kg_ref,
                 q_ref, k_ref, v_ref, u_ref, ga_ref, gs_ref):
    h = _rms(x_ref[...], nw_ref[...]).astype(BF16)
    q = _dot(h, w_ref[:, 0:Q_W])
    qms = _dot((q * q).astype(BF16), gq_ref[...])
    q_ref[...] = (q * lax.rsqrt(qms + RMS_EPS) * qg_ref[...]).astype(BF16)
    kv = _dot(h, w_ref[:, Q_W:U0])
    k = kv[:, :KV_W]
    kms = _dot((k * k).astype(BF16), gk_ref[...])
    k_ref[...] = k * lax.rsqrt(kms + RMS_EPS) * kg_ref[...]
    v_ref[...] = kv[:, KV_W:]
    u_ref[...] = _dot(h, w_ref[:, U0:GA0])
    ga_ref[...] = jax.nn.sigmoid(_dot(h, w_ref[:, GA0:GS0])).astype(BF16)
    gs_ref[...] = jax.nn.sigmoid(_dot(h, w_ref[:, GS0:IN_COLS])).astype(BF16)


def _proj(x, wts):
    bsz, length, _ = x.shape
    tm = min(TM_PROJ, length)
    row = lambda w: pl.BlockSpec((None, tm, w), lambda b, i: (b, i, 0))
    return pl.pallas_call(
        _proj_kernel,
        grid=(bsz, length // tm),
        in_specs=[row(D_MODEL), _full((1, D_MODEL)), _full((D_MODEL, IN_COLS)), _full((Q_W, Q_W)),
                  _full((KV_W, KV_W)), _full((1, Q_W)), _full((1, KV_W))],
        out_specs=[row(Q_W), row(KV_W), row(KV_W),
                   pl.BlockSpec((tm, SSM_WIDTH), lambda b, i: (i, b)), row(D_MODEL), row(D_MODEL)],
        out_shape=[jax.ShapeDtypeStruct((bsz, length, Q_W), BF16),
                   jax.ShapeDtypeStruct((bsz, length, KV_W), F32),
                   jax.ShapeDtypeStruct((bsz, length, KV_W), F32),
                   jax.ShapeDtypeStruct((length, bsz * SSM_WIDTH), F32),
                   jax.ShapeDtypeStruct((bsz, length, D_MODEL), BF16),
                   jax.ShapeDtypeStruct((bsz, length, D_MODEL), BF16)],
        compiler_params=_params("parallel", "parallel"),
        name="proj",
    )(x, wts["norm_mix"], wts["w_in"], wts["gq"], wts["gk"], wts["q_gain"], wts["k_gain"])


def _softmax_parts(s, sink):
    m = jnp.maximum(jnp.max(s, axis=-1, keepdims=True), sink)
    e = jnp.exp(s - m)
    den = jnp.sum(e, axis=-1, keepdims=True) + jnp.exp(sink - m)
    return e, den


def _attn_prompt_kernel(sink_ref, q_ref, kc_ref, kp_ref, vc_ref, vp_ref, bias_ref, o_ref):
    lane_k = lax.broadcasted_iota(I32, (2 * BLOCK, KV_W), 1)
    low_k = lane_k < HEAD_DIM
    low_o = lax.broadcasted_iota(I32, (BLOCK, 2 * HEAD_DIM), 1) < HEAD_DIM

    def padded(prev_ref, cur_ref):
        band = jnp.concatenate([prev_ref[...], cur_ref[...]], axis=0)
        rolled = pltpu.roll(band, HEAD_DIM, axis=1)
        a = [jnp.where(low_k, band, 0.0).astype(BF16), jnp.where(low_k, rolled, 0.0).astype(BF16)]
        b = [jnp.where(low_k, 0.0, rolled).astype(BF16), jnp.where(low_k, 0.0, band).astype(BF16)]
        return a, b

    ka, kb = padded(kp_ref, kc_ref)
    va, vb = padded(vp_ref, vc_ref)
    for p in range(N_HEADS // 2):
        kv = (2 * p) // GQA
        qp = q_ref[:, p * 2 * HEAD_DIM:(p + 1) * 2 * HEAD_DIM]
        ea, da = _softmax_parts(_dot_nt(qp, ka[kv]) + bias_ref[2 * p], sink_ref[2 * p])
        eb, db = _softmax_parts(_dot_nt(qp, kb[kv]) + bias_ref[2 * p + 1], sink_ref[2 * p + 1])
        o = _dot(ea.astype(BF16), va[kv]) + _dot(eb.astype(BF16), vb[kv])
        o_ref[:, p * 2 * HEAD_DIM:(p + 1) * 2 * HEAD_DIM] = (o / jnp.where(low_o, da, db)).astype(BF16)


def _attn_prompt(q, k, v, bias, sinks):
    bsz, length, _ = q.shape
    cur = lambda w: pl.BlockSpec((None, BLOCK, w), lambda b, j: (b, j, 0))
    prev = lambda w: pl.BlockSpec((None, BLOCK, w), lambda b, j: (b, jnp.maximum(j - 1, 0), 0))
    return pl.pallas_call(
        _attn_prompt_kernel,
        grid=(bsz, length // BLOCK),
        in_specs=[pl.BlockSpec(memory_space=pltpu.SMEM), cur(Q_W), cur(KV_W), prev(KV_W), cur(KV_W), prev(KV_W),
                  pl.BlockSpec((None, N_HEADS, BLOCK, 2 * BLOCK), lambda b, j: (jnp.minimum(j, 1), 0, 0, 0))],
        out_specs=cur(Q_W),
        out_shape=jax.ShapeDtypeStruct((bsz, length, Q_W), BF16),
        compiler_params=_params("parallel", "parallel"),
        name="attn_prompt",
    )(sinks, q, k, k, v, v, bias)


def _attn_sample_kernel(q_ref, k_ref, v_ref, bias_ref, sink_ref, o_ref):
    s = jnp.einsum("bqd,bkd->bqk", q_ref[...], k_ref[...], preferred_element_type=F32) + bias_ref[...]
    e, den = _softmax_parts(s, sink_ref[...])
    o = jnp.einsum("bqk,bkd->bqd", e.astype(BF16), v_ref[...], preferred_element_type=F32)
    o_ref[...] = (o / den).astype(BF16)


def _attn_sample(q, kk, vv, bias, sink):
    nb, rows, _ = q.shape
    blk = SEQ_BLK * N_KV_HEADS
    spec = lambda r, c: pl.BlockSpec((blk, r, c), lambda i: (i, 0, 0))
    return pl.pallas_call(
        _attn_sample_kernel,
        grid=(nb // blk,),
        in_specs=[spec(rows, HEAD_DIM), spec(2 * WINDOW, HEAD_DIM), spec(2 * WINDOW, HEAD_DIM),
                  _full((blk, rows, 2 * WINDOW)), _full((blk, rows, 1))],
        out_specs=spec(rows, HEAD_DIM),
        out_shape=jax.ShapeDtypeStruct((nb, rows, HEAD_DIM), BF16),
        compiler_params=_params("parallel"),
        name="attn_sample",
    )(q, kk, vv, bias, sink)


def _ssm_tail(y, u, d_ref, wglu_ref):
    z = jax.nn.gelu(y + d_ref[...] * u)
    return (z * jax.nn.sigmoid(_dot(z.astype(BF16), wglu_ref[...]))).astype(BF16)


def _ssm_prompt_kernel(u_ref, b0r_ref, b0i_ref, b1r_ref, b1i_ref, ar_ref, ai_ref,
                       c0r_ref, c0i_ref, c1r_ref, c1i_ref, d_ref, wglu_ref,
                       z_ref, hr_out, hi_out, xr_s, xi_s):
    rows = u_ref.shape[0]
    steps = xr_s.shape[1] // SUBLANES

    @pl.when(pl.program_id(0) == 0)
    def _():
        hr_out[...] = jnp.zeros_like(hr_out)
        hi_out[...] = jnp.zeros_like(hi_out)

    u = u_ref[...]
    ub = u.astype(BF16)
    for half, (br_ref, bi_ref) in enumerate(((b0r_ref, b0i_ref), (b1r_ref, b1i_ref))):
        uh = ub[:, half * (SSM_WIDTH // 2):(half + 1) * (SSM_WIDTH // 2)]
        xr = _dot(uh, br_ref[...])
        xi = _dot(uh, bi_ref[...])
        for s in range(N_SLABS):
            xr_s[s, pl.ds(half, rows, stride=2), :] = xr[:, s * LANES:(s + 1) * LANES]
            xi_s[s, pl.ds(half, rows, stride=2), :] = xi[:, s * LANES:(s + 1) * LANES]

    def step(t, carry):
        hr, hi = carry
        r0 = pl.multiple_of(t * SUBLANES, SUBLANES)
        nr, ni = [], []
        for s in range(N_SLABS):
            ar, ai = ar_ref[s], ai_ref[s]
            r = ar * hr[s] - ai * hi[s] + xr_s[s, pl.ds(r0, SUBLANES), :]
            i = ar * hi[s] + ai * hr[s] + xi_s[s, pl.ds(r0, SUBLANES), :]
            xr_s[s, pl.ds(r0, SUBLANES), :] = r
            xi_s[s, pl.ds(r0, SUBLANES), :] = i
            nr.append(r)
            ni.append(i)
        return tuple(nr), tuple(ni)

    h0 = (tuple(hr_out[s] for s in range(N_SLABS)), tuple(hi_out[s] for s in range(N_SLABS)))
    hr, hi = lax.fori_loop(0, steps, step, h0, unroll=2)
    for s in range(N_SLABS):
        hr_out[s] = hr[s]
        hi_out[s] = hi[s]

    ys = []
    for half, (cr_ref, ci_ref) in enumerate(((c0r_ref, c0i_ref), (c1r_ref, c1i_ref))):
        gather = lambda ref: jnp.concatenate(
            [ref[s, pl.ds(half, rows, stride=2), :] for s in range(N_SLABS)], axis=-1).astype(BF16)
        ys.append(_dot(gather(xr_s), cr_ref[...]) - _dot(gather(xi_s), ci_ref[...]))
    z_ref[...] = _ssm_tail(jnp.concatenate(ys, axis=-1), u, d_ref, wglu_ref)


def _ssm_prompt(u, bsz, wts):
    assert 2 * bsz == SUBLANES
    rows = T_SCAN * bsz
    n_chunks = u.shape[0] // rows
    hw = SSM_WIDTH // 2
    state = jax.ShapeDtypeStruct((N_SLABS, SUBLANES, LANES), F32)
    slab_spec = _full((N_SLABS, SUBLANES, LANES))
    return pl.pallas_call(
        _ssm_prompt_kernel,
        grid=(n_chunks,),
        in_specs=[pl.BlockSpec((rows, SSM_WIDTH), lambda i: (i, 0))]
        + [_full((hw, HALF_STATE))] * 4 + [slab_spec] * 2 + [_full((HALF_STATE, hw))] * 4
        + [_full((1, SSM_WIDTH)), _full((SSM_WIDTH, SSM_WIDTH))],
        out_specs=[pl.BlockSpec((rows, SSM_WIDTH), lambda i: (i, 0)), slab_spec, slab_spec],
        out_shape=[jax.ShapeDtypeStruct(u.shape, BF16), state, state],
        scratch_shapes=[pltpu.VMEM((N_SLABS, 2 * rows, LANES), F32)] * 2,
        compiler_params=_params("arbitrary"),
        name="ssm_prompt",
    )(u, wts["b0r"], wts["b0i"], wts["b1r"], wts["b1i"], wts["a_slab_r"], wts["a_slab_i"],
      wts["c0r"], wts["c0i"], wts["c1r"], wts["c1i"], wts["ssm_d"], wts["w_ssm_glu"])


def _ssm_sample_kernel(steps, u_ref, h0r_ref, h0i_ref, br_ref, bi_ref, ar_ref, ai_ref, cr_ref, ci_ref,
                       d_ref, wglu_ref, z_ref, hr_out, hi_out):
    nseq = h0r_ref.shape[0]
    u = u_ref[...]
    ub = u.astype(BF16)
    xr = _dot(ub, br_ref[...])
    xi = _dot(ub, bi_ref[...])
    ar, ai = ar_ref[...], ai_ref[...]
    hr, hi = h0r_ref[...], h0i_ref[...]
    hrs, his = [], []
    for t in range(steps):
        hr, hi = (ar * hr - ai * hi + xr[t * nseq:(t + 1) * nseq],
                  ar * hi + ai * hr + xi[t * nseq:(t + 1) * nseq])
        hrs.append(hr)
        his.append(hi)
    hr_out[...] = hr
    hi_out[...] = hi
    y = (_dot(jnp.concatenate(hrs, axis=0).astype(BF16), cr_ref[...])
         - _dot(jnp.concatenate(his, axis=0).astype(BF16), ci_ref[...]))
    z_ref[...] = _ssm_tail(y, u, d_ref, wglu_ref)


def _ssm_sample(u, h0r, h0i, wts):
    nseq = h0r.shape[0]
    steps = u.shape[0] // nseq
    state = jax.ShapeDtypeStruct((nseq, N_STATE), F32)
    return pl.pallas_call(
        functools.partial(_ssm_sample_kernel, steps),
        grid=(1,),
        in_specs=[_full(u.shape), _full(h0r.shape), _full(h0i.shape), _full((SSM_WIDTH, N_STATE)),
                  _full((SSM_WIDTH, N_STATE)), _full((1, N_STATE)), _full((1, N_STATE)),
                  _full((N_STATE, SSM_WIDTH)), _full((N_STATE, SSM_WIDTH)), _full((1, SSM_WIDTH)),
                  _full((SSM_WIDTH, SSM_WIDTH))],
        out_specs=[_full(u.shape), _full(h0r.shape), _full(h0r.shape)],
        out_shape=[jax.ShapeDtypeStruct(u.shape, BF16), state, state],
        compiler_params=_params("arbitrary"),
        name="ssm_sample",
    )(u, h0r, h0i, wts["bd_r"], wts["bd_i"], wts["a_r"], wts["a_i"], wts["cd_r"], wts["cd_i"],
      wts["ssm_d"], wts["w_ssm_glu"])


def _merge_router_kernel(x_ref, o_ref, z_ref, ga_ref, gs_ref, wa_ref, ws_ref, wo_ref, nm_ref,
                         wrh_ref, wrl_ref, br_ref, tri_ref, c0_ref,
                         xm_ref, h2_ref, ti_ref, tp_ref, rk_ref, cnt_ref):
    tm = x_ref.shape[0]

    @pl.when((pl.program_id(0) == 0) & (pl.program_id(1) == 0))
    def _():
        cnt_ref[...] = c0_ref[...]

    a = _dot(o_ref[...], wa_ref[...])
    s = _dot(z_ref[...], ws_ref[...])
    merged = (ga_ref[...].astype(F32) * a + gs_ref[...].astype(F32) * s).astype(BF16)
    xm = x_ref[...] + _dot(merged, wo_ref[...])
    xm_ref[...] = xm
    h2 = _rms(xm, nm_ref[...])
    hh = h2.astype(BF16)
    hl = (h2 - hh.astype(F32)).astype(BF16)
    h2_ref[...] = _pack_bf16_pairs(hh)
    lg = _dot_nt(wrh_ref[...], hh) + _dot_nt(wrh_ref[...], hl) + _dot_nt(wrl_ref[...], hh) + br_ref[...]
    eio = lax.broadcasted_iota(I32, (N_EXPERTS, tm), 0)
    vals, idxs = [], []
    for _ in range(TOP_K):
        m = jnp.max(lg, axis=0, keepdims=True)
        ix = jnp.min(jnp.where(lg == m, eio, N_EXPERTS), axis=0, keepdims=True)
        vals.append(m)
        idxs.append(ix)
        lg = jnp.where(eio == ix, -jnp.inf, lg)
    es = [jnp.exp(v - vals[0]) for v in vals]
    den = es[0] + es[1] + es[2] + es[3]
    hit = [eio == ix for ix in idxs]
    mask = sum(jnp.where(h, 1.0, 0.0) for h in hit)
    cum = _dot(mask.astype(BF16), tri_ref[...]) + cnt_ref[...]
    ti_ref[...] = jnp.concatenate(idxs, axis=0)
    tp_ref[...] = jnp.concatenate([e / den for e in es], axis=0)
    rk_ref[...] = jnp.concatenate(
        [jnp.sum(jnp.where(h, cum, 0.0), axis=0, keepdims=True) for h in hit], axis=0).astype(I32)
    cnt_ref[...] = cnt_ref[...] + jnp.sum(mask, axis=1, keepdims=True)


def _merge_router(x, o, z, ga, gs, counts0, wts):
    bsz, length, _ = x.shape
    tm = min(TM_PROJ, length)
    nl = length // tm
    n = bsz * length
    row = lambda w: pl.BlockSpec((None, tm, w), lambda b, i: (b, i, 0))
    flat = lambda r: pl.BlockSpec((r, tm), lambda b, i: (0, b * nl + i))
    return pl.pallas_call(
        _merge_router_kernel,
        grid=(bsz, nl),
        in_specs=[row(D_MODEL), row(Q_W), pl.BlockSpec((tm, SSM_WIDTH), lambda b, i: (i, b)), row(D_MODEL),
                  row(D_MODEL), _full((Q_W, D_MODEL)), _full((SSM_WIDTH, D_MODEL)), _full((D_MODEL, D_MODEL)),
                  _full((1, D_MODEL)), _full((N_EXPERTS, D_MODEL)), _full((N_EXPERTS, D_MODEL)),
                  _full((N_EXPERTS, 1)), _full((tm, tm)), _full((N_EXPERTS, 1))],
        out_specs=[row(D_MODEL), pl.BlockSpec((tm, PACK_W), lambda b, i: (b * nl + i, 0)),
                   flat(TOP_K), flat(TOP_K), flat(TOP_K), _full((N_EXPERTS, 1))],
        out_shape=[jax.ShapeDtypeStruct((bsz, length, D_MODEL), F32),
                   jax.ShapeDtypeStruct((n, PACK_W), U32),
                   jax.ShapeDtypeStruct((TOP_K, n), I32),
                   jax.ShapeDtypeStruct((TOP_K, n), F32),
                   jax.ShapeDtypeStruct((TOP_K, n), I32),
                   jax.ShapeDtypeStruct((N_EXPERTS, 1), F32)],
        compiler_params=_params("arbitrary", "arbitrary"),
        name="merge_router",
    )(x, o, z, ga, gs, wts["w_attn_branch"], wts["w_ssm_branch"], wts["w_out"], wts["norm_moe"],
      wts["wr_hi"], wts["wr_lo"], wts["b_router"], wts["tri"][:tm, :tm], counts0)


def _plan_kernel(starts_ref, ti_ref, rk_ref, dest_ref):
    ti = ti_ref[...]
    acc = rk_ref[...]
    for e in range(N_EXPERTS):
        acc = acc + jnp.where(ti == e, starts_ref[e], 0)
    dest_ref[...] = acc


def _plan(starts, ti, rk):
    return pl.pallas_call(
        _plan_kernel,
        grid=(1,),
        in_specs=[pl.BlockSpec(memory_space=pltpu.SMEM), _full(ti.shape), _full(rk.shape)],
        out_specs=_full(ti.shape),
        out_shape=jax.ShapeDtypeStruct(ti.shape, I32),
        name="plan",
    )(starts, ti, rk)


def _tile_major(dest, tile):
    k, n = dest.shape
    return dest.reshape(k, n // tile, tile).transpose(1, 0, 2).reshape(n // tile, 1, k * tile)


def _dispatch_kernel(dest_ref, h_ref, xs_in_ref, xs_ref, sem):
    del xs_in_ref
    tile = h_ref.shape[0]
    for k in range(TOP_K):
        def issue(t, carry, k=k):
            d = dest_ref[0, k * tile + t]
            pltpu.make_async_copy(h_ref.at[pl.ds(t, 1)], xs_ref.at[pl.ds(d, 1)], sem).start()
            return carry
        lax.fori_loop(0, tile, issue, 0, unroll=8)
    for k in range(TOP_K):
        pltpu.make_async_copy(h_ref, xs_ref.at[pl.ds(0, tile)], sem).wait()


def _dispatch(dest, h2, xs):
    n = h2.shape[0]
    tile = min(T_ROWS, n)
    return pl.pallas_call(
        _dispatch_kernel,
        grid=(n // tile,),
        in_specs=[pl.BlockSpec((None, 1, TOP_K * tile), lambda i: (i, 0, 0), memory_space=pltpu.SMEM),
                  pl.BlockSpec((tile, PACK_W), lambda i: (i, 0)), pl.BlockSpec(memory_space=pl.ANY)],
        out_specs=pl.BlockSpec(memory_space=pl.ANY),
        out_shape=jax.ShapeDtypeStruct(xs.shape, xs.dtype),
        scratch_shapes=[pltpu.SemaphoreType.DMA],
        input_output_aliases={2: 0},
        compiler_params=_params("arbitrary"),
        name="dispatch",
    )(_tile_major(dest, tile), h2, xs)


def _expert_kernel(te_ref, nu_ref, xs_ref, wg_ref, bg_ref, wu_ref, bu_ref, wd_ref, bd_ref, ys_ref,
                   wg_b, wu_b, wd_b):
    i = pl.program_id(0)
    live = i < nu_ref[0]
    fresh = (i == 0) | (te_ref[i] != te_ref[jnp.maximum(i - 1, 0)])

    @pl.when(live & fresh)
    def _():
        wg_b[...] = wg_ref[...].astype(BF16)
        wu_b[...] = wu_ref[...].astype(BF16)
        wd_b[...] = wd_ref[...].astype(BF16)

    @pl.when(live)
    def _():
        x = _unpack_bf16_pairs(xs_ref[...]).astype(BF16)
        g = jnp.minimum(_dot(x, wg_b[...]) + bg_ref[...], SWIGLU_LIMIT)
        up = jnp.clip(_dot(x, wu_b[...]) + bu_ref[...], -SWIGLU_LIMIT, SWIGLU_LIMIT)
        act = g * jax.nn.sigmoid(SWIGLU_ALPHA * g) * (up + 1.0)
        ys_ref[...] = _pack_bf16_pairs(_dot(act.astype(BF16), wd_b[...]) + bd_ref[...])

    @pl.when(jnp.logical_not(live))
    def _():
        ys_ref[...] = jnp.zeros_like(ys_ref)


def _experts(tile_expert, n_used, xs, w_gate, b_gate, w_up, b_up, w_down, b_down):
    n_tiles = xs.shape[0] // T_ROWS
    wspec = pl.BlockSpec((None, D_MODEL, D_FF), lambda i, te, nu: (te[i], 0, 0))
    bspec = pl.BlockSpec((None, 1, D_FF), lambda i, te, nu: (te[i], 0, 0))
    rows = pl.BlockSpec((T_ROWS, PACK_W), lambda i, te, nu: (i, 0))
    return pl.pallas_call(
        _expert_kernel,
        grid_spec=pltpu.PrefetchScalarGridSpec(
            num_scalar_prefetch=2,
            grid=(n_tiles,),
            in_specs=[rows, wspec, bspec, wspec, bspec, wspec, bspec],
            out_specs=rows,
            scratch_shapes=[pltpu.VMEM((D_MODEL, D_FF), BF16)] * 3),
        out_shape=jax.ShapeDtypeStruct(xs.shape, U32),
        compiler_params=_params("arbitrary"),
        name="experts",
    )(tile_expert, n_used, xs, w_gate, b_gate, w_up, b_up, w_down, b_down)


def _combine_kernel(dest_ref, tp_ref, xm_ref, p_ref, np_ref, wg_ref, wp_ref, ys_ref, out_ref, ybuf, sem):
    tile = xm_ref.shape[0]
    for k in range(TOP_K):
        def issue(t, carry, k=k):
            d = dest_ref[0, k * tile + t]
            pltpu.make_async_copy(ys_ref.at[pl.ds(d, 1)], ybuf.at[k, pl.ds(t, 1)], sem).start()
            return carry
        lax.fori_loop(0, tile, issue, 0, unroll=8)
    for k in range(TOP_K):
        pltpu.make_async_copy(ys_ref.at[pl.ds(0, tile)], ybuf.at[k], sem).wait()
    tp = tp_ref[...]
    moe = tp[:, 0:1] * _unpack_bf16_pairs(ybuf[0])
    for k in range(1, TOP_K):
        moe = moe + tp[:, k:k + 1] * _unpack_bf16_pairs(ybuf[k])
    x2 = xm_ref[...] + moe
    gate = jax.nn.sigmoid(_dot(_rms(x2, np_ref[...]).astype(BF16), wg_ref[...]))
    out_ref[...] = x2 + _dot(p_ref[...].astype(BF16), wp_ref[...]) * gate


def _combine(dest, tp, xm, p, ys, wts):
    n = xm.shape[0]
    tile = min(T_ROWS, n)
    rows = lambda w: pl.BlockSpec((tile, w), lambda i: (i, 0))
    return pl.pallas_call(
        _combine_kernel,
        grid=(n // tile,),
        in_specs=[pl.BlockSpec((None, 1, TOP_K * tile), lambda i: (i, 0, 0), memory_space=pltpu.SMEM),
                  rows(TOP_K), rows(D_MODEL), rows(PLE_DIM), _full((1, D_MODEL)), _full((D_MODEL, D_MODEL)),
                  _full((PLE_DIM, D_MODEL)), pl.BlockSpec(memory_space=pl.ANY)],
        out_specs=rows(D_MODEL),
        out_shape=jax.ShapeDtypeStruct((n, D_MODEL), F32),
        scratch_shapes=[pltpu.VMEM((TOP_K, tile, PACK_W), U32), pltpu.SemaphoreType.DMA],
        compiler_params=_params("arbitrary"),
        name="combine",
    )(_tile_major(dest, tile), tp, xm, p, wts["norm_ple"], wts["w_ple_gate"], wts["w_ple_proj"], ys)


def _alibi_slopes():
    return jnp.exp2(-8.0 * (jnp.arange(N_HEADS, dtype=F32) + 1.0) / N_HEADS)


def _prompt_bias():
    qi = jnp.arange(BLOCK)[:, None]
    kj = jnp.arange(2 * BLOCK)[None, :]
    dist = BLOCK + qi - kj
    ok = (dist >= 0) & (dist <= WINDOW)
    b = -_alibi_slopes()[:, None, None] * dist.astype(F32)[None]
    later = jnp.where(ok[None], b, -jnp.inf)
    first = jnp.where((ok & (kj >= BLOCK))[None], b, -jnp.inf)
    return jnp.stack([first, later])


def _sample_bias(steps, sinks):
    t = jnp.repeat(jnp.arange(steps), GQA)[:, None]
    kj = jnp.arange(2 * WINDOW)[None, :]
    dist = WINDOW + t - kj
    ok = (dist >= 0) & (dist <= WINDOW) & (kj < WINDOW + steps)
    slopes = _alibi_slopes().reshape(N_KV_HEADS, 1, GQA)
    slopes = jnp.broadcast_to(slopes, (N_KV_HEADS, steps, GQA)).reshape(N_KV_HEADS, steps * GQA, 1)
    bias = jnp.where(ok[None], -slopes * dist.astype(F32)[None], -jnp.inf)
    sink = jnp.broadcast_to(sinks.reshape(N_KV_HEADS, 1, GQA), (N_KV_HEADS, steps, GQA))
    return bias, sink.reshape(N_KV_HEADS, steps * GQA, 1)


def _ssm_discretise(a_re, a_im, log_dt, b_re, b_im):
    dt = jnp.exp(log_dt)[:, None]
    mag = jnp.exp(dt * a_re)
    ab_re = mag * jnp.cos(dt * a_im)
    ab_im = mag * jnp.sin(dt * a_im)
    den = a_re * a_re + a_im * a_im
    nr = ab_re - 1.0
    f_re = ((nr * a_re + ab_im * a_im) / den)[..., None]
    f_im = ((ab_im * a_re - nr * a_im) / den)[..., None]
    return ab_re, ab_im, f_re * b_re - f_im * b_im, f_re * b_im + f_im * b_re


def _block_diag_in(bb):
    eye = jnp.eye(SSM_GROUPS, dtype=bb.dtype)
    return jnp.einsum("gnc,gh->gchn", bb, eye).reshape(SSM_WIDTH, N_STATE)


def _block_diag_out(cc):
    eye = jnp.eye(SSM_GROUPS, dtype=cc.dtype)
    return jnp.einsum("gcn,gh->gnhc", cc, eye).reshape(N_STATE, SSM_WIDTH)


def _prepare(norm_mix, w_in, q_norm, k_norm, w_attn_branch, ssm_a_re, ssm_a_im, ssm_log_dt, ssm_b_re, ssm_b_im,
             ssm_c_re, ssm_c_im, ssm_d, w_ssm_glu, w_ssm_branch, w_out, norm_moe, w_router, b_router,
             norm_ple, w_ple_gate, w_ple_proj):
    head_mean = jnp.kron(jnp.eye(N_HEADS, dtype=F32), jnp.full((HEAD_DIM, HEAD_DIM), 1.0 / HEAD_DIM, F32))
    ab_re, ab_im, bb_re, bb_im = _ssm_discretise(ssm_a_re, ssm_a_im, ssm_log_dt, ssm_b_re, ssm_b_im)
    bd_r, bd_i = _block_diag_in(bb_re).astype(BF16), _block_diag_in(bb_im).astype(BF16)
    cd_r, cd_i = _block_diag_out(ssm_c_re).astype(BF16), _block_diag_out(ssm_c_im).astype(BF16)
    a_r, a_i = ab_re.reshape(1, N_STATE), ab_im.reshape(1, N_STATE)
    hw = SSM_WIDTH // 2

    def slabs(a):
        halves = a.reshape(2, N_SLABS, LANES).transpose(1, 0, 2)
        return jnp.tile(halves, (1, SUBLANES // 2, 1))

    wr_hi = w_router.T.astype(BF16)
    return dict(
        norm_mix=norm_mix.reshape(1, D_MODEL), w_in=w_in.astype(BF16),
        gq=head_mean.astype(BF16), gk=head_mean[:KV_W, :KV_W].astype(BF16),
        q_gain=jnp.tile(q_norm, N_HEADS).reshape(1, Q_W) * HEAD_DIM ** -0.5,
        k_gain=jnp.tile(k_norm, N_KV_HEADS).reshape(1, KV_W),
        b0r=bd_r[:hw, :HALF_STATE], b0i=bd_i[:hw, :HALF_STATE], b1r=bd_r[hw:, HALF_STATE:], b1i=bd_i[hw:, HALF_STATE:],
        c0r=cd_r[:HALF_STATE, :hw], c0i=cd_i[:HALF_STATE, :hw], c1r=cd_r[HALF_STATE:, hw:], c1i=cd_i[HALF_STATE:, hw:],
        a_slab_r=slabs(a_r), a_slab_i=slabs(a_i), a_r=a_r, a_i=a_i, bd_r=bd_r, bd_i=bd_i, cd_r=cd_r, cd_i=cd_i,
        ssm_d=ssm_d.reshape(1, SSM_WIDTH), w_ssm_glu=w_ssm_glu.astype(BF16),
        w_attn_branch=w_attn_branch.astype(BF16), w_ssm_branch=w_ssm_branch.astype(BF16), w_out=w_out.astype(BF16),
        norm_moe=norm_moe.reshape(1, D_MODEL), wr_hi=wr_hi,
        wr_lo=(w_router.T - wr_hi.astype(F32)).astype(BF16), b_router=b_router.reshape(N_EXPERTS, 1),
        tri=jnp.triu(jnp.ones((TM_PROJ, TM_PROJ), F32), 1).astype(BF16),
        norm_ple=norm_ple.reshape(1, D_MODEL), w_ple_gate=w_ple_gate.astype(BF16), w_ple_proj=w_ple_proj.astype(BF16),
    )


def _group_metadata(counts, n_tiles):
    cnt = counts[:, 0].astype(I32)
    padded = ((cnt + T_ROWS - 1) // T_ROWS) * T_ROWS
    ends = jnp.cumsum(padded)
    n_used = ends[-1] // T_ROWS
    tile_row = jnp.arange(n_tiles, dtype=I32) * T_ROWS
    te = jnp.minimum(jnp.sum(tile_row[:, None] >= ends[None, :], axis=1), N_EXPERTS - 1).astype(I32)
    last = te[jnp.maximum(n_used - 1, 0)]
    te = jnp.where(jnp.arange(n_tiles) < n_used, te, last)
    return ends - padded, te, n_used.reshape(1).astype(I32)


def kernel(x_prompt, x_sample, p_prompt, p_sample, cache_k, cache_v, state_ssm_re, state_ssm_im, norm_mix, w_in, q_norm, k_norm, attn_sinks, w_attn_branch, ssm_a_re, ssm_a_im, ssm_log_dt, ssm_b_re, ssm_b_im, ssm_c_re, ssm_c_im, ssm_d, w_ssm_glu, w_ssm_branch, w_out, norm_moe, w_router, b_router, w_gate, b_gate, w_up, b_up, w_down, b_down, norm_ple, w_ple_gate, w_ple_proj):
    depth = norm_mix.shape[0]
    assert depth == 1
    bsz, length, _ = x_prompt.shape
    nseq, steps, _ = x_sample.shape
    n_p, n_s = bsz * length, nseq * steps
    wts = _prepare(norm_mix[0], w_in[0], q_norm[0], k_norm[0], w_attn_branch[0], ssm_a_re[0], ssm_a_im[0],
                   ssm_log_dt[0], ssm_b_re[0], ssm_b_im[0], ssm_c_re[0], ssm_c_im[0], ssm_d[0], w_ssm_glu[0],
                   w_ssm_branch[0], w_out[0], norm_moe[0], w_router[0], b_router[0], norm_ple[0],
                   w_ple_gate[0], w_ple_proj[0])
    sinks = attn_sinks[0]

    q_p, k_p, v_p, u_p, ga_p, gs_p = _proj(x_prompt, wts)
    o_p = _attn_prompt(q_p, k_p, v_p, _prompt_bias(), sinks)
    z_p, hr_p, hi_p = _ssm_prompt(u_p.reshape(length * bsz, SSM_WIDTH), bsz, wts)
    z_p = z_p.reshape(length, bsz * SSM_WIDTH)

    q_s, k_s, v_s, u_s, ga_s, gs_s = _proj(x_sample.reshape(1, n_s, D_MODEL), wts)
    kk = jnp.concatenate([cache_k[0], k_s.reshape(nseq, steps, N_KV_HEADS, HEAD_DIM)], axis=1)
    vv = jnp.concatenate([cache_v[0], v_s.reshape(nseq, steps, N_KV_HEADS, HEAD_DIM)], axis=1)

    def keys(a):
        a = jnp.pad(a, ((0, 0), (0, WINDOW - steps), (0, 0), (0, 0)))
        return a.transpose(0, 2, 1, 3).reshape(nseq * N_KV_HEADS, 2 * WINDOW, HEAD_DIM).astype(BF16)

    qs = q_s.reshape(nseq, steps, N_KV_HEADS, GQA, HEAD_DIM).transpose(0, 2, 1, 3, 4)
    bias_s, sink_s = _sample_bias(steps, sinks)
    o_s = _attn_sample(qs.reshape(nseq * N_KV_HEADS, steps * GQA, HEAD_DIM), keys(kk), keys(vv),
                       jnp.tile(bias_s, (SEQ_BLK, 1, 1)), jnp.tile(sink_s, (SEQ_BLK, 1, 1)))
    o_s = o_s.reshape(nseq, N_KV_HEADS, steps, GQA, HEAD_DIM).transpose(0, 2, 1, 3, 4).reshape(1, n_s, Q_W)
    u_ts = u_s.reshape(nseq, steps, SSM_WIDTH).transpose(1, 0, 2).reshape(n_s, SSM_WIDTH)
    z_ts, hr_s, hi_s = _ssm_sample(u_ts, state_ssm_re[0].reshape(nseq, N_STATE),
                                   state_ssm_im[0].reshape(nseq, N_STATE), wts)
    z_s = z_ts.reshape(steps, nseq, SSM_WIDTH).transpose(1, 0, 2).reshape(n_s, SSM_WIDTH)

    zero_counts = jnp.zeros((N_EXPERTS, 1), F32)
    xm_p, h2_p, ti_p, tp_p, rk_p, counts = _merge_router(x_prompt, o_p, z_p, ga_p, gs_p, zero_counts, wts)
    xm_s, h2_s, ti_s, tp_s, rk_s, counts = _merge_router(
        x_sample.reshape(1, n_s, D_MODEL), o_s, z_s, ga_s, gs_s, counts, wts)

    n_rows = -(-((n_p + n_s) * TOP_K + N_EXPERTS * (T_ROWS - 1)) // T_ROWS) * T_ROWS
    starts, tile_expert, n_used = _group_metadata(counts, n_rows // T_ROWS)
    dest = _plan(starts, jnp.concatenate([ti_p, ti_s], axis=1), jnp.concatenate([rk_p, rk_s], axis=1))
    dest_p, dest_s = dest[:, :n_p], dest[:, n_p:]
    xs = jnp.zeros((n_rows, PACK_W), U32)
    xs = _dispatch(dest_p, h2_p, xs)
    xs = _dispatch(dest_s, h2_s, xs)
    ys = _experts(tile_expert, n_used, xs, w_gate[0], b_gate[0].reshape(N_EXPERTS, 1, D_FF), w_up[0],
                  b_up[0].reshape(N_EXPERTS, 1, D_FF), w_down[0], b_down[0].reshape(N_EXPERTS, 1, D_MODEL))
    y_p = _combine(dest_p, tp_p.T, xm_p.reshape(n_p, D_MODEL), p_prompt[0].reshape(n_p, PLE_DIM), ys, wts)
    y_s = _combine(dest_s, tp_s.T, xm_s.reshape(n_s, D_MODEL), p_sample[0].reshape(n_s, PLE_DIM), ys, wts)

    def cache_tail(a):
        return a[:, -WINDOW:].reshape(1, a.shape[0], WINDOW, N_KV_HEADS, HEAD_DIM)

    def slab_state(h):
        h = h.reshape(N_SLABS, bsz, 2, LANES).transpose(1, 2, 0, 3)
        return h.reshape(1, bsz, SSM_GROUPS, SSM_STATE)

    return (y_p.reshape(bsz, length, D_MODEL), y_s.reshape(nseq, steps, D_MODEL),
            cache_tail(k_p), cache_tail(v_p), slab_state(hr_p), slab_state(hi_p),
            kk[None, :, -WINDOW:], vv[None, :, -WINDOW:],
            hr_s.reshape(1, nseq, SSM_GROUPS, SSM_STATE), hi_s.reshape(1, nseq, SSM_GROUPS, SSM_STATE))
```
